```python
import jax, jax.numpy as jnp
from jax import lax
import numpy as np

D_MODEL = 2048
BATCH = 2
SEQ = 4096
DEPTH = 1
DEC_BATCH = 128
DEC_SEQ = 1
PAST_LEN = 2048
PAGE_SIZE = 128

HEAD_DIM = 128
N_HEADS = D_MODEL // HEAD_DIM
H_SB = N_HEADS // 2
H_FOX = N_HEADS - H_SB
MIX_WIDTH = N_HEADS * HEAD_DIM
SB_WIDTH = H_SB * HEAD_DIM
FOX_WIDTH = H_FOX * HEAD_DIM
IN_COLS = 3 * MIX_WIDTH + H_FOX
D_FF = 256 * ((8 * D_MODEL // 3 + 255) // 256)
Q_BLOCK = 128
EPS = 1e-6
FORGET_BIAS = 2.0

kernel_name = "hybrid_stickbreak_fox_macaron_step"


def rms_norm(x, g):
    xf = x.astype(jnp.float32)
    y = xf * lax.rsqrt(jnp.mean(xf * xf, axis=-1, keepdims=True) + EPS)
    return (y * g.astype(jnp.float32)).astype(x.dtype)


def swiglu(h, w1, w3, w2):
    return (jax.nn.silu(h @ w1) * (h @ w3)) @ w2


def project(h, w_in, b_f):
    b, t, _ = h.shape
    proj = h @ w_in
    q = proj[..., :MIX_WIDTH].reshape(b, t, N_HEADS, HEAD_DIM)
    k = proj[..., MIX_WIDTH:2 * MIX_WIDTH].reshape(b, t, N_HEADS, HEAD_DIM)
    v = proj[..., 2 * MIX_WIDTH:3 * MIX_WIDTH].reshape(b, t, N_HEADS, HEAD_DIM)
    logf = jax.nn.log_sigmoid((proj[..., 3 * MIX_WIDTH:] + b_f).astype(jnp.float32))
    return q, k, v, logf


def stick_breaking(q, k, v, q_pos, k_pos):
    z = jnp.einsum('bqhd,bkhd->bhqk', q, k).astype(jnp.float32) * (HEAD_DIM ** -0.5)
    mask = k_pos[None, :] < q_pos[:, None]
    log_keep = jnp.where(mask, jax.nn.log_sigmoid(-z), 0.0)
    log_rest = lax.cumsum(log_keep, axis=3, reverse=True) - log_keep
    a = jnp.where(mask, jnp.exp(jax.nn.log_sigmoid(z) + log_rest), 0.0)
    return jnp.einsum('bhqk,bkhd->bqhd', a.astype(v.dtype), v)


def forgetting_attention(q, k, v, cq, ck, q_pos, k_pos):
    z = jnp.einsum('bqhd,bkhd->bhqk', q, k).astype(jnp.float32) * (HEAD_DIM ** -0.5)
    z = z + jnp.swapaxes(cq, 1, 2)[:, :, :, None] - jnp.swapaxes(ck, 1, 2)[:, :, None, :]
    mask = k_pos[None, :] <= q_pos[:, None]
    p = jax.nn.softmax(jnp.where(mask, z, -jnp.inf), axis=-1)
    return jnp.einsum('bhqk,bkhd->bqhd', p.astype(v.dtype), v)


def attend(q, k, v, cq, ck, q_pos, k_pos):
    o_sb = stick_breaking(q[:, :, :H_SB], k[:, :, :H_SB], v[:, :, :H_SB], q_pos, k_pos)
    o_fox = forgetting_attention(q[:, :, H_SB:], k[:, :, H_SB:], v[:, :, H_SB:], cq, ck, q_pos, k_pos)
    return o_sb, o_fox


def prompt_mixing(q, k, v, logf):
    b, s = q.shape[:2]
    n_blk = s // Q_BLOCK
    c = jnp.cumsum(logf, axis=1)
    pos = jnp.arange(s, dtype=jnp.int32)

    def to_blocks(a):
        return jnp.swapaxes(a.reshape((b, n_blk, Q_BLOCK) + a.shape[2:]), 0, 1)

    def from_blocks(a):
        return jnp.swapaxes(a, 0, 1).reshape((b, s) + a.shape[3:])

    def one_block(args):
        qb, cqb, pb = args
        return attend(qb, k, v, cqb, c, pb, pos)

    o_sb, o_fox = lax.map(one_block, (to_blocks(q), to_blocks(c), pos.reshape(n_blk, Q_BLOCK)))
    return from_blocks(o_sb), from_blocks(o_fox)


def sample_mixing(q, k_new, v_new, logf_new, k_cache, v_cache, logf_cache, page_table):
    b, t = q.shape[:2]
    past = page_table.shape[1] * k_cache.shape[1]

    def gather(cache):
        g = cache[page_table]
        return g.reshape((b, past) + cache.shape[2:])

    k = jnp.concatenate([gather(k_cache).astype(k_new.dtype), k_new], axis=1)
    v = jnp.concatenate([gather(v_cache).astype(v_new.dtype), v_new], axis=1)
    logf = jnp.concatenate([gather(logf_cache).astype(jnp.float32), logf_new], axis=1)
    c = jnp.cumsum(logf, axis=1)
    pos = jnp.arange(past + t, dtype=jnp.int32)
    return attend(q, k, v, c[:, past:], c, pos[past:], pos)


def merge_heads(o_sb, o_fox, g_sb, g_fox, w_o):
    b, t = o_sb.shape[:2]
    o_sb = rms_norm(o_sb.reshape(b, t, SB_WIDTH), g_sb)
    o_fox = rms_norm(o_fox.reshape(b, t, FOX_WIDTH), g_fox)
    return jnp.concatenate([o_sb, o_fox], axis=-1) @ w_o


def decoder_layer(x, p, past):
    (g_ffn1, w1a, w3a, w2a, g_mix, w_in, b_f, g_sb, g_fox, w_o, g_ffn2, w1b, w3b, w2b) = p
    x = x + 0.5 * swiglu(rms_norm(x, g_ffn1), w1a, w3a, w2a)
    q, k, v, logf = project(rms_norm(x, g_mix), w_in, b_f)
    if past is None:
        o_sb, o_fox = prompt_mixing(q, k, v, logf)
    else:
        o_sb, o_fox = sample_mixing(q, k, v, logf, *past)
    x = x + merge_heads(o_sb, o_fox, g_sb, g_fox, w_o)
    x = x + 0.5 * swiglu(rms_norm(x, g_ffn2), w1b, w3b, w2b)
    return x, k, v, logf


def setup_inputs(seed: int = 0) -> dict:
    key = jax.random.key(seed)
    ks = jax.random.split(key, 24)
    n_pages = PAST_LEN // PAGE_SIZE
    n_used = DEC_BATCH * n_pages
    n_pool = n_used + max(1, n_used // 4)

    def nrm(k, shape, scale=1.0):
        return scale * jax.random.normal(k, shape, jnp.float32)

    def gain(k, shape):
        return 1.0 + 0.05 * jax.random.normal(k, shape, jnp.float32)

    page_table = jax.random.permutation(ks[5], n_pool)[:n_used].reshape(DEC_BATCH, n_pages).astype(jnp.int32)
    return {
        "x_prompt": nrm(ks[0], (BATCH, SEQ, D_MODEL)),
        "x_sample": nrm(ks[1], (DEC_BATCH, DEC_SEQ, D_MODEL)),
        "cache_k": nrm(ks[2], (DEPTH, n_pool, PAGE_SIZE, N_HEADS, HEAD_DIM)),
        "cache_v": nrm(ks[3], (DEPTH, n_pool, PAGE_SIZE, N_HEADS, HEAD_DIM)),
        "cache_logf": jax.nn.log_sigmoid(FORGET_BIAS + nrm(ks[4], (DEPTH, n_pool, PAGE_SIZE, H_FOX))),
        "page_table": page_table,
        "g_ffn1": gain(ks[6], (DEPTH, D_MODEL)),
        "w1_ffn1": nrm(ks[7], (DEPTH, D_MODEL, D_FF), D_MODEL ** -0.5),
        "w3_ffn1": nrm(ks[8], (DEPTH, D_MODEL, D_FF), D_MODEL ** -0.5),
        "w2_ffn1": nrm(ks[9], (DEPTH, D_FF, D_MODEL), D_FF ** -0.5),
        "g_mix": gain(ks[10], (DEPTH, D_MODEL)),
        "w_in": nrm(ks[11], (DEPTH, D_MODEL, IN_COLS), D_MODEL ** -0.5),
        "b_f": FORGET_BIAS + 0.5 * jax.random.normal(ks[12], (DEPTH, H_FOX), jnp.float32),
        "g_sb": gain(ks[13], (DEPTH, SB_WIDTH)),
        "g_fox": gain(ks[14], (DEPTH, FOX_WIDTH)),
        "w_o": nrm(ks[15], (DEPTH, MIX_WIDTH, D_MODEL), MIX_WIDTH ** -0.5),
        "g_ffn2": gain(ks[16], (DEPTH, D_MODEL)),
        "w1_ffn2": nrm(ks[17], (DEPTH, D_MODEL, D_FF), D_MODEL ** -0.5),
        "w3_ffn2": nrm(ks[18], (DEPTH, D_MODEL, D_FF), D_MODEL ** -0.5),
        "w2_ffn2": nrm(ks[19], (DEPTH, D_FF, D_MODEL), D_FF ** -0.5),
        "g_final": gain(ks[20], (D_MODEL,)),
    }


def reference(x_prompt, x_sample, cache_k, cache_v, cache_logf, page_table,
              g_ffn1, w1_ffn1, w3_ffn1, w2_ffn1, g_mix, w_in, b_f, g_sb, g_fox, w_o,
              g_ffn2, w1_ffn2, w3_ffn2, w2_ffn2, g_final):
    yp, ys = x_prompt, x_sample
    kp, vp, fp, k_s, v_s, f_s = [], [], [], [], [], []
    for l in range(DEPTH):
        p = (g_ffn1[l], w1_ffn1[l], w3_ffn1[l], w2_ffn1[l], g_mix[l], w_in[l], b_f[l],
             g_sb[l], g_fox[l], w_o[l], g_ffn2[l], w1_ffn2[l], w3_ffn2[l], w2_ffn2[l])
        yp, k, v, lf = decoder_layer(yp, p, None)
        kp.append(k); vp.append(v); fp.append(lf.astype(cache_logf.dtype))
        ys, k, v, lf = decoder_layer(ys, p, (cache_k[l], cache_v[l], cache_logf[l], page_table))
        k_s.append(k); v_s.append(v); f_s.append(lf.astype(cache_logf.dtype))
    y_prompt = rms_norm(yp, g_final)
    y_sample = rms_norm(ys, g_final)
    return (y_prompt, y_sample, jnp.stack(kp), jnp.stack(vp), jnp.stack(fp),
            jnp.stack(k_s), jnp.stack(v_s), jnp.stack(f_s))
```

```python
import functools
import math

import jax
import jax.numpy as jnp
from jax import lax
from jax.experimental import pallas as pl
from jax.experimental.pallas import tpu as pltpu

F32 = jnp.float32
BF16 = jnp.bfloat16
EPS = 1e-6
LANES = 128
VMEM_LIMIT_CAP = 60 * 1024 * 1024


def _params(semantics, vmem_bytes):
    limit = int(min(VMEM_LIMIT_CAP, max(32 * 1024 * 1024, vmem_bytes * 5 // 4 + (4 << 20))))
    return pltpu.CompilerParams(dimension_semantics=semantics, vmem_limit_bytes=limit)


def _rms(x, g):
    return x * lax.rsqrt(jnp.mean(x * x, axis=-1, keepdims=True) + EPS) * g


def _log_sigmoid(x):
    return jnp.minimum(x, 0.0) - jnp.log1p(jnp.exp(-jnp.abs(x)))


def _split_bf16(x, parts):
    out = []
    for _ in range(parts - 1):
        hi = x.astype(BF16)
        out.append(hi)
        x = x - hi.astype(F32)
    out.append(x.astype(BF16))
    return out


def _ffn_kernel(x_ref, g_ref, w1_ref, w3_ref, w2_ref, gf_ref, o_ref, h_ref, acc_ref, *, final_norm):
    j = pl.program_id(1)

    @pl.when(j == 0)
    def _():
        h_ref[...] = _rms(x_ref[...], g_ref[...]).astype(BF16)
        acc_ref[...] = jnp.zeros_like(acc_ref)

    h = h_ref[...]
    a = jnp.dot(h, w1_ref[...], preferred_element_type=F32)
    b = jnp.dot(h, w3_ref[...], preferred_element_type=F32)
    u = (a * jax.nn.sigmoid(a)) * b
    acc_ref[...] += jnp.dot(u.astype(BF16), w2_ref[...], preferred_element_type=F32)

    @pl.when(j == pl.num_programs(1) - 1)
    def _():
        y = x_ref[...] + 0.5 * acc_ref[...]
        if final_norm:
            y = _rms(y, gf_ref[...])
        o_ref[...] = y


def _ffn(x, g, w1, w3, w2, g_final, *, tm, tf):
    m, d = x.shape
    ff = w1.shape[1]
    assert m % tm == 0 and ff % tf == 0
    final_norm = g_final is not None
    gf = g_final if final_norm else g
    vmem = 2 * 2 * tm * d * 4 + tm * d * (2 + 4) + 2 * 3 * d * tf * 2 + 6 * tm * tf * 4
    return pl.pallas_call(
        functools.partial(_ffn_kernel, final_norm=final_norm),
        grid=(m // tm, ff // tf),
        in_specs=[
            pl.BlockSpec((tm, d), lambda i, j: (i, 0)),
            pl.BlockSpec((1, d), lambda i, j: (0, 0)),
            pl.BlockSpec((d, tf), lambda i, j: (0, j)),
            pl.BlockSpec((d, tf), lambda i, j: (0, j)),
            pl.BlockSpec((tf, d), lambda i, j: (j, 0)),
            pl.BlockSpec((1, d), lambda i, j: (0, 0)),
        ],
        out_specs=pl.BlockSpec((tm, d), lambda i, j: (i, 0)),
        out_shape=jax.ShapeDtypeStruct((m, d), F32),
        scratch_shapes=[pltpu.VMEM((tm, d), BF16), pltpu.VMEM((tm, d), F32)],
        compiler_params=_params(("arbitrary", "arbitrary"), vmem),
        name="ffn",
    )(x, g.reshape(1, d), w1, w3, w2, gf.reshape(1, d))


def _proj_kernel(x_ref, g_ref, wq_ref, wk_ref, wv_ref, wf_ref, bf_ref, tri_ref,
                 q_ref, k_ref, v_ref, kb_ref, vb_ref, lf_ref, c_ref, h_ref, carry_ref, *, tiles_per_seq):
    i = pl.program_id(0)
    j = pl.program_id(1)

    @pl.when(j == 0)
    def _():
        h = _rms(x_ref[...], g_ref[...]).astype(BF16)
        h_ref[...] = h
        lf = _log_sigmoid(jnp.dot(h, wf_ref[...], preferred_element_type=F32) + bf_ref[...])
        lf_ref[...] = lf

        @pl.when(i % tiles_per_seq == 0)
        def _():
            carry_ref[...] = jnp.zeros_like(carry_ref)

        tri = tri_ref[...]
        c = carry_ref[...]
        for part in _split_bf16(lf, 3):
            c = c + jnp.dot(tri, part, preferred_element_type=F32)
        c_ref[...] = c
        carry_ref[...] = c[c.shape[0] - 1:, :]

    h = h_ref[...]
    q_ref[...] = jnp.dot(h, wq_ref[...], preferred_element_type=F32).astype(BF16)
    k = jnp.dot(h, wk_ref[...], preferred_element_type=F32)
    k_ref[...] = k
    kb_ref[...] = k.astype(BF16)
    v = jnp.dot(h, wv_ref[...], preferred_element_type=F32)
    v_ref[...] = v
    vb_ref[...] = v.astype(BF16)


def _proj(x, g, w_qkv, w_f, b_f, *, tm, tn, seq_len):
    m, d = x.shape
    mix = w_qkv.shape[1] // 3
    assert m % tm == 0 and mix % tn == 0 and seq_len % tm == 0
    nj = mix // tn
    tri = (lax.broadcasted_iota(jnp.int32, (tm, tm), 1) <= lax.broadcasted_iota(jnp.int32, (tm, tm), 0)).astype(BF16)
    row_blk = pl.BlockSpec((tm, tn), lambda i, j: (i, j))
    pad_blk = pl.BlockSpec((tm, LANES), lambda i, j: (i, 0))
    vmem = 2 * tm * d * 4 + tm * d * 2 + 2 * 3 * d * tn * 2 + 2 * tm * tn * (2 + 4 + 4 + 2 + 2) + 2 * tm * tm * 2 + (4 << 20)
    return pl.pallas_call(
        functools.partial(_proj_kernel, tiles_per_seq=seq_len // tm),
        grid=(m // tm, nj),
        in_specs=[
            pl.BlockSpec((tm, d), lambda i, j: (i, 0)),
            pl.BlockSpec((1, d), lambda i, j: (0, 0)),
            pl.BlockSpec((d, tn), lambda i, j: (0, j)),
            pl.BlockSpec((d, tn), lambda i, j: (0, nj + j)),
            pl.BlockSpec((d, tn), lambda i, j: (0, 2 * nj + j)),
            pl.BlockSpec((d, LANES), lambda i, j: (0, 0)),
            pl.BlockSpec((1, LANES), lambda i, j: (0, 0)),
            pl.BlockSpec((tm, tm), lambda i, j: (0, 0)),
        ],
        out_specs=[row_blk, row_blk, row_blk, row_blk, row_blk, pad_blk, pad_blk],
        out_shape=[
            jax.ShapeDtypeStruct((m, mix), BF16),
            jax.ShapeDtypeStruct((m, mix), F32),
            jax.ShapeDtypeStruct((m, mix), F32),
            jax.ShapeDtypeStruct((m, mix), BF16),
            jax.ShapeDtypeStruct((m, mix), BF16),
            jax.ShapeDtypeStruct((m, LANES), F32),
            jax.ShapeDtypeStruct((m, LANES), F32),
        ],
        scratch_shapes=[pltpu.VMEM((tm, d), BF16), pltpu.VMEM((1, LANES), F32)],
        compiler_params=_params(("arbitrary", "arbitrary"), vmem),
        name="proj",
    )(x, g.reshape(1, d), w_qkv, w_qkv, w_qkv, w_f, b_f, tri)


def _fox_kernel(q_ref, k_ref, v_ref, c_ref, ct_ref, o_ref, *, blk, scale):
    h = pl.program_id(1)
    i = pl.program_id(2)
    q = q_ref[0]
    cblk = c_ref[0]
    lane = lax.broadcasted_iota(jnp.int32, cblk.shape, 1)
    cq = jnp.sum(jnp.where(lane == h, cblk, 0.0), axis=1, keepdims=True)

    def logits(j):
        k0 = pl.multiple_of(j * blk, blk)
        s = lax.dot_general(q, k_ref[0, pl.ds(k0, blk), :], (((1,), (1,)), ((), ())),
                            preferred_element_type=F32)
        ck = ct_ref[0, pl.ds(h, 1), pl.ds(k0, blk)]
        return s * scale + (cq - ck), k0

    def update(carry, z, k0):
        m, l, acc = carry
        m_new = jnp.maximum(m, jnp.max(z, axis=1, keepdims=True))
        alpha = jnp.exp(m - m_new)
        p = jnp.exp(z - m_new)
        l = alpha * l + jnp.sum(p, axis=1, keepdims=True)
        pv = jnp.dot(p.astype(BF16), v_ref[0, pl.ds(k0, blk), :], preferred_element_type=F32)
        return m_new, l, alpha * acc + pv

    def body(j, carry):
        z, k0 = logits(j)
        return update(carry, z, k0)

    hd = q.shape[1]
    init = (jnp.full((blk, 1), -jnp.inf, F32), jnp.zeros((blk, 1), F32), jnp.zeros((blk, hd), F32))
    carry = lax.fori_loop(0, i, body, init)
    z, k0 = logits(i)
    row = lax.broadcasted_iota(jnp.int32, z.shape, 0)
    col = lax.broadcasted_iota(jnp.int32, z.shape, 1)
    _, l, acc = update(carry, jnp.where(col <= row, z, -jnp.inf), k0)
    o_ref[0] = acc / l


def _sb_kernel(q_ref, k_ref, v_ref, u_ref, o_ref, *, blk, scale):
    i = pl.program_id(2)
    q = q_ref[0]
    u = u_ref[...]

    def block(j, rest, acc, masked):
        k0 = pl.multiple_of(j * blk, blk)
        z = lax.dot_general(q, k_ref[0, pl.ds(k0, blk), :], (((1,), (1,)), ((), ())),
                            preferred_element_type=F32) * scale
        log_beta = _log_sigmoid(z)
        log_keep = log_beta - z
        if masked:
            row = lax.broadcasted_iota(jnp.int32, z.shape, 0)
            col = lax.broadcasted_iota(jnp.int32, z.shape, 1)
            valid = col < row
            log_keep = jnp.where(valid, log_keep, 0.0)
        hi, lo = _split_bf16(log_keep, 2)
        suffix = jnp.dot(hi, u, preferred_element_type=F32) + jnp.dot(lo, u, preferred_element_type=F32)
        a = jnp.exp(log_beta + (rest + suffix))
        if masked:
            a = jnp.where(valid, a, 0.0)
        acc = acc + jnp.dot(a.astype(BF16), v_ref[0, pl.ds(k0, blk), :], preferred_element_type=F32)
        rest = rest + jnp.sum(log_keep, axis=1, keepdims=True)
        return rest, acc

    hd = q.shape[1]
    rest, acc = block(i, jnp.zeros((blk, 1), F32), jnp.zeros((blk, hd), F32), True)

    def body(jj, carry):
        return block(i - 1 - jj, carry[0], carry[1], False)

    _, acc = lax.fori_loop(0, i, body, (rest, acc))
    o_ref[0] = acc


def _prompt_attention(q, kb, vb, c, ct, u, *, h_sb, hd, blk):
    b, s, mix = q.shape
    n_heads = mix // hd
    h_fox = n_heads - h_sb
    assert s % blk == 0
    scale = hd ** -0.5
    nq = s // blk
    semantics = ("arbitrary", "arbitrary", "arbitrary")
    vmem = 2 * 2 * s * hd * 2 + 8 * blk * hd * 4 + 16 * blk * blk * 4 + (4 << 20)

    def qo_spec(h0):
        return pl.BlockSpec((1, blk, hd), lambda bi, h, i: (bi, i, h0 + h))

    def kv_spec(h0):
        return pl.BlockSpec((1, s, hd), lambda bi, h, i: (bi, 0, h0 + h))

    o_sb = pl.pallas_call(
        functools.partial(_sb_kernel, blk=blk, scale=scale),
        grid=(b, h_sb, nq),
        in_specs=[qo_spec(0), kv_spec(0), kv_spec(0), pl.BlockSpec((blk, blk), lambda bi, h, i: (0, 0))],
        out_specs=pl.BlockSpec((1, blk, hd), lambda bi, h, i: (bi, i, h)),
        out_shape=jax.ShapeDtypeStruct((b, s, h_sb * hd), F32),
        compiler_params=_params(semantics, vmem),
        name="sb_attn",
    )(q, kb, vb, u)
    o_fox = pl.pallas_call(
        functools.partial(_fox_kernel, blk=blk, scale=scale),
        grid=(b, h_fox, nq),
        in_specs=[qo_spec(h_sb), kv_spec(h_sb), kv_spec(h_sb),
                  pl.BlockSpec((1, blk, LANES), lambda bi, h, i: (bi, i, 0)),
                  pl.BlockSpec((1, ct.shape[1], s), lambda bi, h, i: (bi, 0, 0))],
        out_specs=pl.BlockSpec((1, blk, hd), lambda bi, h, i: (bi, i, h)),
        out_shape=jax.ShapeDtypeStruct((b, s, h_fox * hd), F32),
        compiler_params=_params(semantics, vmem),
        name="fox_attn",
    )(q, kb, vb, c, ct)
    return o_sb, o_fox


def _decode_kernel(pt_ref, q_ref, kn_ref, vn_ref, lfn_ref, *refs, pages, h_sb, hd_shift, scale):
    del pt_ref
    k_refs = refs[:pages]
    v_refs = refs[pages:2 * pages]
    lf_refs = refs[2 * pages:3 * pages]
    u_ref, o_ref, kbuf, vbuf, acc_ref, m_ref, l_ref, rest_ref, restf_ref = refs[3 * pages:]
    c = pl.program_id(1)
    n_heads, mix = acc_ref.shape
    page = k_refs[0].shape[1]

    head_of_col = lax.shift_right_logical(lax.broadcasted_iota(jnp.int32, (n_heads, mix), 1), hd_shift)
    head_of_row = lax.broadcasted_iota(jnp.int32, (n_heads, mix), 0)
    own = head_of_col == head_of_row
    is_fox_row = head_of_row >= h_sb
    qbd32 = jnp.where(own, jnp.broadcast_to(q_ref[0].astype(F32), (n_heads, mix)), 0.0)
    qbd = qbd32.astype(BF16)

    @pl.when(c == 0)
    def _():
        kn = kn_ref[0].astype(BF16).astype(F32)
        vn = vn_ref[0].astype(BF16).astype(F32)
        z_self = jnp.sum(qbd32 * kn, axis=1, keepdims=True) * scale
        m_ref[...] = z_self[h_sb:]
        l_ref[...] = jnp.ones_like(l_ref)
        acc_ref[...] = jnp.where(is_fox_row, jnp.broadcast_to(vn, (n_heads, mix)), 0.0)
        rest_ref[...] = jnp.zeros_like(rest_ref)
        restf_ref[...] = lfn_ref[0][:, :1]

    for r in range(pages):
        kbuf[r * page:(r + 1) * page, :] = k_refs[r][0].astype(BF16)
        vbuf[r * page:(r + 1) * page, :] = v_refs[r][0].astype(BF16)
    s = lax.dot_general(qbd, kbuf[...], (((1,), (1,)), ((), ())), preferred_element_type=F32) * scale
    lf = jnp.concatenate([lf_refs[r][0] for r in range(pages)], axis=1)

    z_sb = s[:h_sb]
    log_beta = _log_sigmoid(z_sb)
    log_keep = log_beta - z_sb
    parts = _split_bf16(jnp.concatenate([log_keep, lf], axis=0), 3)
    d = jnp.dot(jnp.concatenate(parts, axis=0), u_ref[...], preferred_element_type=F32)
    suffix = d[:n_heads] + d[n_heads:2 * n_heads] + d[2 * n_heads:]
    a = jnp.exp(log_beta + (rest_ref[...] + suffix[:h_sb]))
    rest_ref[...] += jnp.sum(log_keep, axis=1, keepdims=True)

    z_fox = s[h_sb:] + (restf_ref[...] + suffix[h_sb:])
    m_old = m_ref[...]
    m_new = jnp.maximum(m_old, jnp.max(z_fox, axis=1, keepdims=True))
    alpha = jnp.exp(m_old - m_new)
    p = jnp.exp(z_fox - m_new)
    l_ref[...] = alpha * l_ref[...] + jnp.sum(p, axis=1, keepdims=True)
    m_ref[...] = m_new
    restf_ref[...] += jnp.sum(lf, axis=1, keepdims=True)

    w = jnp.concatenate([a, p], axis=0).astype(BF16)
    pv = jnp.dot(w, vbuf[...], preferred_element_type=F32)
    row_scale = jnp.concatenate([jnp.ones((h_sb, 1), F32), alpha], axis=0)
    acc_ref[...] = acc_ref[...] * row_scale + pv

    @pl.when(c == pl.num_programs(1) - 1)
    def _():
        norm = jnp.concatenate([jnp.ones((h_sb, 1), F32), l_ref[...]], axis=0)
        o_ref[0] = jnp.sum(jnp.where(own, acc_ref[...] / norm, 0.0), axis=0, keepdims=True)


def _decode_attention(q, k_new, v_new, lf_new, cache_k, cache_v, cache_lft, page_table, *, h_sb, hd, pages):
    nb, mix = q.shape
    n_heads = mix // hd
    h_fox = n_heads - h_sb
    n_pool, page = cache_k.shape[0], cache_k.shape[1]
    n_pages = page_table.shape[1]
    assert n_pages % pages == 0 and h_sb % 8 == 0 and h_fox % 8 == 0 and (1 << int(math.log2(hd))) == hd
    n_chunks = n_pages // pages
    chunk = pages * page
    u = (lax.broadcasted_iota(jnp.int32, (chunk, chunk), 0) > lax.broadcasted_iota(jnp.int32, (chunk, chunk), 1)).astype(BF16)
    lfn = jnp.broadcast_to(lf_new[:, :h_fox, None], (nb, h_fox, LANES))

    def page_map(r):
        return lambda b, c, pt: (pt[b * n_pages + (n_chunks - 1 - c) * pages + r], 0, 0)

    row = pl.BlockSpec((1, 1, mix), lambda b, c, pt: (b, 0, 0))
    in_specs = [row, row, row, pl.BlockSpec((1, h_fox, LANES), lambda b, c, pt: (b, 0, 0))]
    in_specs += [pl.BlockSpec((1, page, mix), page_map(r)) for r in range(pages)]
    in_specs += [pl.BlockSpec((1, page, mix), page_map(r)) for r in range(pages)]
    in_specs += [pl.BlockSpec((1, h_fox, page), page_map(r)) for r in range(pages)]
    in_specs += [pl.BlockSpec((chunk, chunk), lambda b, c, pt: (0, 0))]
    vmem = 2 * 2 * pages * page * mix * 4 + 2 * chunk * mix * 2 + 2 * chunk * chunk * 2 + (8 << 20)
    out = pl.pallas_call(
        functools.partial(_decode_kernel, pages=pages, h_sb=h_sb, hd_shift=int(math.log2(hd)), scale=hd ** -0.5),
        grid_spec=pltpu.PrefetchScalarGridSpec(
            num_scalar_prefetch=1,
            grid=(nb, n_chunks),
            in_specs=in_specs,
            out_specs=pl.BlockSpec((1, 1, mix), lambda b, c, pt: (b, 0, 0)),
            scratch_shapes=[
                pltpu.VMEM((chunk, mix), BF16), pltpu.VMEM((chunk, mix), BF16),
                pltpu.VMEM((n_heads, mix), F32),
                pltpu.VMEM((h_fox, 1), F32), pltpu.VMEM((h_fox, 1), F32),
                pltpu.VMEM((h_sb, 1), F32), pltpu.VMEM((h_fox, 1), F32),
            ],
        ),
        out_shape=jax.ShapeDtypeStruct((nb, 1, mix), F32),
        compiler_params=_params(("arbitrary", "arbitrary"), vmem),
        name="decode_attn",
    )(page_table.reshape(-1), q.reshape(nb, 1, mix), k_new.reshape(nb, 1, mix), v_new.reshape(nb, 1, mix), lfn,
      *([cache_k] * pages), *([cache_v] * pages), *([cache_lft] * pages), u)
    return out.reshape(nb, mix)


def _merge_kernel(x_ref, osb_ref, ofox_ref, gsb_ref, gfox_ref, wo_ref, o_ref, h_ref):
    @pl.when(pl.program_id(1) == 0)
    def _():
        w_sb = osb_ref.shape[1]
        h_ref[:, :w_sb] = _rms(osb_ref[...], gsb_ref[...]).astype(BF16)
        h_ref[:, w_sb:] = _rms(ofox_ref[...], gfox_ref[...]).astype(BF16)

    o_ref[...] = x_ref[...] + jnp.dot(h_ref[...], wo_ref[...], preferred_element_type=F32)


def _merge(x, o_sb, o_fox, g_sb, g_fox, w_o, *, tm, tn):
    m, d = x.shape
    w_sb, w_fox = o_sb.shape[1], o_fox.shape[1]
    mix = w_sb + w_fox
    assert m % tm == 0 and d % tn == 0
    vmem = 2 * tm * mix * 4 + tm * mix * 2 + 2 * mix * tn * 2 + 4 * tm * tn * 4 + (4 << 20)
    return pl.pallas_call(
        _merge_kernel,
        grid=(m // tm, d // tn),
        in_specs=[
            pl.BlockSpec((tm, tn), lambda i, j: (i, j)),
            pl.BlockSpec((tm, w_sb), lambda i, j: (i, 0)),
            pl.BlockSpec((tm, w_fox), lambda i, j: (i, 0)),
            pl.BlockSpec((1, w_sb), lambda i, j: (0, 0)),
            pl.BlockSpec((1, w_fox), lambda i, j: (0, 0)),
            pl.BlockSpec((mix, tn), lambda i, j: (0, j)),
        ],
        out_specs=pl.BlockSpec((tm, tn), lambda i, j: (i, j)),
        out_shape=jax.ShapeDtypeStruct((m, d), F32),
        scratch_shapes=[pltpu.VMEM((tm, mix), BF16)],
        compiler_params=_params(("arbitrary", "arbitrary"), vmem),
        name="merge",
    )(x, o_sb, o_fox, g_sb.reshape(1, w_sb), g_fox.reshape(1, w_fox), w_o)


def _row_tile(m, cap):
    t = min(m, cap)
    while m % t:
        t //= 2
    return t


FFN_ROWS, FFN_COLS = 512, 512
PROJ_ROWS, PROJ_COLS = 512, 512
MERGE_ROWS, MERGE_COLS = 512, 512
ATTN_BLOCK = 256
DECODE_PAGES = 4


def kernel(x_prompt, x_sample, cache_k, cache_v, cache_logf, page_table, g_ffn1, w1_ffn1, w3_ffn1, w2_ffn1, g_mix, w_in, b_f, g_sb, g_fox, w_o, g_ffn2, w1_ffn2, w3_ffn2, w2_ffn2, g_final):
    depth = g_ffn1.shape[0]
    batch, seq, d = x_prompt.shape
    nb, dec_seq, _ = x_sample.shape
    assert dec_seq == 1
    n_heads, hd = cache_k.shape[3], cache_k.shape[4]
    h_fox = cache_logf.shape[3]
    h_sb = n_heads - h_fox
    mix = n_heads * hd
    n_pool, page = cache_k.shape[1], cache_k.shape[2]
    assert hd == LANES and w_in.shape[2] == 3 * mix + h_fox

    blk = ATTN_BLOCK
    u_blk = (lax.broadcasted_iota(jnp.int32, (blk, blk), 0) > lax.broadcasted_iota(jnp.int32, (blk, blk), 1)).astype(BF16)

    yp = x_prompt.reshape(batch * seq, d)
    ys = x_sample.reshape(nb, d)
    outs = {name: [] for name in ("kp", "vp", "fp", "ks", "vs", "fs")}
    for l in range(depth):
        last = l == depth - 1
        w1a, w3a, w2a = w1_ffn1[l].astype(BF16), w3_ffn1[l].astype(BF16), w2_ffn1[l].astype(BF16)
        w1b, w3b, w2b = w1_ffn2[l].astype(BF16), w3_ffn2[l].astype(BF16), w2_ffn2[l].astype(BF16)
        w_qkv = w_in[l, :, :3 * mix].astype(BF16)
        w_f = jnp.pad(w_in[l, :, 3 * mix:], ((0, 0), (0, LANES - h_fox))).astype(BF16)
        bf_pad = jnp.pad(b_f[l], (0, LANES - h_fox)).reshape(1, LANES)
        wo = w_o[l].astype(BF16)
        gf = g_final if last else None

        def ffn(x, g, w1, w3, w2, g_fin):
            return _ffn(x, g, w1, w3, w2, g_fin, tm=_row_tile(x.shape[0], FFN_ROWS), tf=FFN_COLS)

        yp = ffn(yp, g_ffn1[l], w1a, w3a, w2a, None)
        q, k, v, kb, vb, lf, c = _proj(yp, g_mix[l], w_qkv, w_f, bf_pad,
                                        tm=_row_tile(seq, PROJ_ROWS), tn=PROJ_COLS, seq_len=seq)
        c3 = c.reshape(batch, seq, LANES)
        ct = jnp.swapaxes(c3[:, :, :h_fox], 1, 2)
        o_sb, o_fox = _prompt_attention(q.reshape(batch, seq, mix), kb.reshape(batch, seq, mix),
                                        vb.reshape(batch, seq, mix), c3, ct, u_blk, h_sb=h_sb, hd=hd, blk=blk)
        yp = _merge(yp, o_sb.reshape(batch * seq, h_sb * hd), o_fox.reshape(batch * seq, h_fox * hd),
                    g_sb[l], g_fox[l], wo, tm=_row_tile(batch * seq, MERGE_ROWS), tn=MERGE_COLS)
        yp = ffn(yp, g_ffn2[l], w1b, w3b, w2b, gf)
        outs["kp"].append(k.reshape(batch, seq, n_heads, hd))
        outs["vp"].append(v.reshape(batch, seq, n_heads, hd))
        outs["fp"].append(lf[:, :h_fox].reshape(batch, seq, h_fox).astype(cache_logf.dtype))

        ys = ffn(ys, g_ffn1[l], w1a, w3a, w2a, None)
        qs, k_s, v_s, _, _, lfs, _ = _proj(ys, g_mix[l], w_qkv, w_f, bf_pad,
                                           tm=_row_tile(nb, PROJ_ROWS), tn=PROJ_COLS, seq_len=_row_tile(nb, PROJ_ROWS))
        o_s = _decode_attention(qs, k_s, v_s, lfs, cache_k[l].reshape(n_pool, page, mix),
                                cache_v[l].reshape(n_pool, page, mix), jnp.swapaxes(cache_logf[l], 1, 2),
                                page_table, h_sb=h_sb, hd=hd, pages=DECODE_PAGES)
        ys = _merge(ys, o_s[:, :h_sb * hd], o_s[:, h_sb * hd:], g_sb[l], g_fox[l], wo,
                    tm=_row_tile(nb, MERGE_ROWS), tn=MERGE_COLS)
        ys = ffn(ys, g_ffn2[l], w1b, w3b, w2b, gf)
        outs["ks"].append(k_s.reshape(nb, dec_seq, n_heads, hd))
        outs["vs"].append(v_s.reshape(nb, dec_seq, n_heads, hd))
        outs["fs"].append(lfs[:, :h_fox].reshape(nb, dec_seq, h_fox).astype(cache_logf.dtype))

    return (yp.reshape(batch, seq, d), ys.reshape(nb, dec_seq, d),
            jnp.stack(outs["kp"]), jnp.stack(outs["vp"]), jnp.stack(outs["fp"]),
            jnp.stack(outs["ks"]), jnp.stack(outs["vs"]), jnp.stack(outs["fs"]))
```
